```python
import jax, jax.numpy as jnp
from jax import lax
import numpy as np

D_MODEL = 1024
BATCH = 8
SEQ = 2048
DEPTH = 1
DEC_BATCH = 128
DEC_SEQ = 1
PAST_LEN = 16384
PAGE_SIZE = 128

MIX_WIDTH = D_MODEL
POOL_WINDOWS = (2, 4, 8, 16)
POOL_GROUPS = len(POOL_WINDOWS)
POOL_WIDTH = MIX_WIDTH // 2
POOL_GC = POOL_WIDTH // POOL_GROUPS
POOL_HIST = max(POOL_WINDOWS) - 1
CONV_WIDTH = MIX_WIDTH - POOL_WIDTH
CONV_HEADS = 8
CONV_K = 3
IN_COLS = POOL_WIDTH + 3 * CONV_WIDTH
D_FF = 2816
PLE_DIM = 256
EPS = 1e-6

kernel_name = "hybrid_pool_shortconv_convffn_step"


def rmsnorm(x, g):
    x32 = x.astype(jnp.float32)
    y = x32 * lax.rsqrt(jnp.mean(x32 * x32, axis=-1, keepdims=True) + EPS)
    return (y * g.astype(jnp.float32)).astype(x.dtype)


def causal_dwconv(z, prev, w, b):
    T = z.shape[1]
    ext = jnp.concatenate([prev.astype(z.dtype), z], axis=1)
    y = sum(ext[:, k:k + T] * w[k] for k in range(CONV_K)) + b
    return y, ext[:, -(CONV_K - 1):]


def pool_mix(u, prev, pos, w_pool, scale):
    N, T, _ = u.shape
    ext = jnp.concatenate([prev.astype(u.dtype), u], axis=1)
    cs = jnp.cumsum(ext.astype(jnp.float32), axis=1)
    cs = jnp.concatenate([jnp.zeros((N, 1, POOL_WIDTH), jnp.float32), cs], axis=1)
    end = cs[:, POOL_HIST + 1:]
    means = []
    for g, w in enumerate(POOL_WINDOWS):
        sl = slice(g * POOL_GC, (g + 1) * POOL_GC)
        start = cs[:, POOL_HIST + 1 - w:POOL_HIST + 1 - w + T, sl]
        cnt = jnp.minimum(w, pos + 1).astype(jnp.float32)[None, :, None]
        means.append((end[..., sl] - start) / cnt)
    d = jnp.concatenate(means, axis=-1) - u.astype(jnp.float32)
    d = d.astype(u.dtype).reshape(N, T, POOL_GROUPS, POOL_GC)
    y = jnp.einsum('ntgc,gcd->ntgd', d, w_pool).reshape(N, T, POOL_WIDTH) * scale
    return y, ext[:, -POOL_HIST:]


def layer(r, p, pool_prev, conv_prev, ffn_prev, pos,
          g_mix, w_in, w_pool, pool_scale, conv_w, conv_b, w_out,
          g_ffn, w_up, ffn_conv_w, ffn_conv_b, w_down, g_ple, w_ple_gate, w_ple_proj):
    h = rmsnorm(r, g_mix)
    proj = h @ w_in
    u = proj[..., :POOL_WIDTH]
    bg, cg, hv = jnp.split(proj[..., POOL_WIDTH:], 3, axis=-1)
    y_pool, pool_new = pool_mix(u, pool_prev, pos, w_pool, pool_scale)
    conv_y, conv_new = causal_dwconv(cg * hv, conv_prev, conv_w, conv_b)
    y_conv = bg * conv_y
    r = r + jnp.concatenate([y_pool, y_conv], axis=-1) @ w_out
    h2 = rmsnorm(r, g_ffn)
    up, ffn_new = causal_dwconv(h2 @ w_up, ffn_prev, ffn_conv_w, ffn_conv_b)
    a, v = jnp.split(up, 2, axis=-1)
    r = r + (jax.nn.silu(a) * v) @ w_down
    gate = jax.nn.sigmoid(rmsnorm(r, g_ple) @ w_ple_gate)
    r = r + gate * (p @ w_ple_proj)
    return r, pool_new, conv_new, ffn_new


def setup_inputs(seed: int = 0) -> dict:
    key = jax.random.key(seed)
    ks = iter(jax.random.split(key, 32))
    f32 = jnp.float32
    nrm = lambda shape, s=1.0: (jax.random.normal(next(ks), shape, f32) * s)
    gain = lambda shape: 1.0 + 0.05 * jax.random.normal(next(ks), shape, f32)
    return {
        "x_prompt": nrm((BATCH, SEQ, D_MODEL)),
        "x_sample": nrm((DEC_BATCH, DEC_SEQ, D_MODEL)),
        "state_pool": nrm((DEPTH, DEC_BATCH, POOL_HIST, POOL_WIDTH)),
        "state_conv": nrm((DEPTH, DEC_BATCH, CONV_K - 1, CONV_WIDTH)),
        "state_ffn": nrm((DEPTH, DEC_BATCH, CONV_K - 1, 2 * D_FF)),
        "p_prompt": nrm((DEPTH, BATCH, SEQ, PLE_DIM)),
        "p_sample": nrm((DEPTH, DEC_BATCH, DEC_SEQ, PLE_DIM)),
        "g_mix": gain((DEPTH, D_MODEL)),
        "w_in": nrm((DEPTH, D_MODEL, IN_COLS), D_MODEL ** -0.5),
        "w_pool": nrm((DEPTH, POOL_GROUPS, POOL_GC, POOL_GC), POOL_GC ** -0.5),
        "pool_scale": gain((DEPTH, POOL_WIDTH)),
        "conv_w": nrm((DEPTH, CONV_K, CONV_WIDTH), CONV_K ** -0.5),
        "conv_b": nrm((DEPTH, CONV_WIDTH), 0.02),
        "w_out": nrm((DEPTH, MIX_WIDTH, D_MODEL), MIX_WIDTH ** -0.5),
        "g_ffn": gain((DEPTH, D_MODEL)),
        "w_up": nrm((DEPTH, D_MODEL, 2 * D_FF), D_MODEL ** -0.5),
        "ffn_conv_w": nrm((DEPTH, CONV_K, 2 * D_FF), CONV_K ** -0.5),
        "ffn_conv_b": nrm((DEPTH, 2 * D_FF), 0.02),
        "w_down": nrm((DEPTH, D_FF, D_MODEL), D_FF ** -0.5),
        "g_ple": gain((DEPTH, D_MODEL)),
        "w_ple_gate": nrm((DEPTH, D_MODEL, D_MODEL), D_MODEL ** -0.5),
        "w_ple_proj": nrm((DEPTH, PLE_DIM, D_MODEL), PLE_DIM ** -0.5),
        "g_final": gain((D_MODEL,)),
    }


def reference(x_prompt, x_sample, state_pool, state_conv, state_ffn, p_prompt, p_sample,
              g_mix, w_in, w_pool, pool_scale, conv_w, conv_b, w_out,
              g_ffn, w_up, ffn_conv_w, ffn_conv_b, w_down, g_ple, w_ple_gate, w_ple_proj, g_final):
    nb, tp = x_prompt.shape[0], x_prompt.shape[1]
    ts = x_sample.shape[1]
    pos_p = jnp.arange(tp, dtype=jnp.int32)
    pos_s = PAST_LEN + jnp.arange(ts, dtype=jnp.int32)
    dt = x_prompt.dtype
    rp, rs = x_prompt, x_sample
    pp_l, pc_l, pf_l, sp_l, sc_l, sf_l = [], [], [], [], [], []
    for i in range(DEPTH):
        w = (g_mix[i], w_in[i], w_pool[i], pool_scale[i], conv_w[i], conv_b[i], w_out[i],
             g_ffn[i], w_up[i], ffn_conv_w[i], ffn_conv_b[i], w_down[i],
             g_ple[i], w_ple_gate[i], w_ple_proj[i])
        rp, pp, pc, pf = layer(rp, p_prompt[i],
                               jnp.zeros((nb, POOL_HIST, POOL_WIDTH), dt),
                               jnp.zeros((nb, CONV_K - 1, CONV_WIDTH), dt),
                               jnp.zeros((nb, CONV_K - 1, 2 * D_FF), dt), pos_p, *w)
        rs, sp, sc, sf = layer(rs, p_sample[i], state_pool[i], state_conv[i], state_ffn[i], pos_s, *w)
        pp_l.append(pp); pc_l.append(pc); pf_l.append(pf)
        sp_l.append(sp); sc_l.append(sc); sf_l.append(sf)
    y_prompt = rmsnorm(rp, g_final)
    y_sample = rmsnorm(rs, g_final)
    return (y_prompt, y_sample,
            jnp.stack(pp_l), jnp.stack(pc_l), jnp.stack(pf_l),
            jnp.stack(sp_l), jnp.stack(sc_l), jnp.stack(sf_l))
```

```python
import functools

import jax
import jax.numpy as jnp
from jax import lax
from jax.experimental import pallas as pl
from jax.experimental.pallas import tpu as pltpu

D_MODEL = 1024
POOL_WINDOWS = (2, 4, 8, 16)
POOL_GC = 128
POOL_WIDTH = 512
POOL_HIST = 15
CONV_WIDTH = 512
CONV_K = 3
D_FF = 2816
PLE_DIM = 256
EPS = 1e-6
PAST_LEN = 16384

FF_CHUNK = 256
N_CHUNK = D_FF // FF_CHUNK
TIME_TILE = 512
POOL_PAD = 16
CONV_PAD = 8
V7X_VMEM_LIMIT_BYTES = 60 * 1024 * 1024


def _rms(x, g):
    ms = jnp.mean(x * x, axis=-1, keepdims=True)
    return x * lax.rsqrt(ms + EPS) * g


def _dot(a, b):
    return jnp.dot(a, b, preferred_element_type=jnp.float32)


def _conv3(ext, w, b):
    return (ext * w[2:3, :] + pltpu.roll(ext, 1, axis=0) * w[1:2, :]
            + pltpu.roll(ext, 2, axis=0) * w[0:1, :] + b)


def _prompt_kernel(x_ref, p_ref, g_mix_ref, w_in_ref, w_pool_ref, pool_scale_ref,
                   conv_w_ref, conv_b_ref, w_out_ref, g_ffn_ref, w_up_ref, fcw_ref, fcb_ref,
                   w_down_ref, g_ple_ref, w_gate_ref, w_proj_ref, g_final_ref,
                   y_ref, npool_ref, nconv_ref, nffn_ref,
                   uext, zext, mix_s, r_s, h2_s, acc_s, upre_s, fcarry):
    tm = x_ref.shape[1]
    j = pl.program_id(1)

    @pl.when(j == 0)
    def _():
        uext[0:POOL_PAD, :] = jnp.zeros((POOL_PAD, POOL_WIDTH), jnp.float32)
        zext[0:CONV_PAD, :] = jnp.zeros((CONV_PAD, CONV_WIDTH), jnp.float32)
        fcarry[...] = jnp.zeros(fcarry.shape, jnp.float32)

    x = x_ref[0]
    h = _rms(x, g_mix_ref[...]).astype(jnp.bfloat16)

    uext[POOL_PAD:, :] = _dot(h, w_in_ref[:, 0:POOL_WIDTH])
    t_glob = j * tm + lax.broadcasted_iota(jnp.int32, (tm, POOL_GC), 0)
    for g, w in enumerate(POOL_WINDOWS):
        sl = slice(g * POOL_GC, (g + 1) * POOL_GC)
        e = uext[:, sl]
        s = e
        k = 1
        while k < w:
            s = s + pltpu.roll(s, k, axis=0)
            k *= 2
        cnt = jnp.minimum(w, t_glob + 1).astype(jnp.float32)
        d = s[POOL_PAD:, :] / cnt - e[POOL_PAD:, :]
        yp = _dot(d.astype(jnp.bfloat16), w_pool_ref[g]) * pool_scale_ref[:, sl]
        mix_s[:, sl] = yp.astype(jnp.bfloat16)

    o = POOL_WIDTH
    bg = _dot(h, w_in_ref[:, o:o + CONV_WIDTH])
    cg = _dot(h, w_in_ref[:, o + CONV_WIDTH:o + 2 * CONV_WIDTH])
    hv = _dot(h, w_in_ref[:, o + 2 * CONV_WIDTH:o + 3 * CONV_WIDTH])
    zext[CONV_PAD:, :] = cg * hv
    conv_y = _conv3(zext[...], conv_w_ref[...], conv_b_ref[...])
    mix_s[:, POOL_WIDTH:] = (bg * conv_y[CONV_PAD:, :]).astype(jnp.bfloat16)

    @pl.when(j == pl.num_programs(1) - 1)
    def _():
        npool_ref[0] = uext[tm + POOL_PAD - POOL_HIST:tm + POOL_PAD, :]
        nconv_ref[0] = zext[tm + CONV_PAD - (CONV_K - 1):tm + CONV_PAD, :]

    uext[0:POOL_PAD, :] = uext[tm:tm + POOL_PAD, :]
    zext[0:CONV_PAD, :] = zext[tm:tm + CONV_PAD, :]

    r1 = x + _dot(mix_s[...], w_out_ref[...])
    r_s[...] = r1
    h2_s[...] = _rms(r1, g_ffn_ref[...]).astype(jnp.bfloat16)

    acc_s[...] = jnp.zeros(acc_s.shape, jnp.float32)

    def ffn_chunk(c, carry):
        upre_s[0:CONV_PAD, :] = fcarry[c]
        upre_s[CONV_PAD:, :] = _dot(h2_s[...], w_up_ref[c])
        up = _conv3(upre_s[...], fcw_ref[c], fcb_ref[c])[CONV_PAD:, :]
        a = up[:, :FF_CHUNK]
        v = up[:, FF_CHUNK:]
        act = (a * jax.nn.sigmoid(a)) * v
        acc_s[...] += _dot(act.astype(jnp.bfloat16), w_down_ref[c])
        fcarry[c] = upre_s[tm:tm + CONV_PAD, :]
        return carry

    lax.fori_loop(0, N_CHUNK, ffn_chunk, 0)

    @pl.when(j == pl.num_programs(1) - 1)
    def _():
        nffn_ref[0] = fcarry[...]

    r2 = r_s[...] + acc_s[...]
    hg = _rms(r2, g_ple_ref[...]).astype(jnp.bfloat16)
    gate = jax.nn.sigmoid(_dot(hg, w_gate_ref[...]))
    pp = _dot(p_ref[0].astype(jnp.bfloat16), w_proj_ref[...])
    r3 = r2 + gate * pp
    y_ref[0] = _rms(r3, g_final_ref[...])


def _sample_kernel(x_ref, p_ref, sp_ref, sc_ref, sf_ref, g_mix_ref, w_in_ref, w_pool_ref,
                   pool_scale_ref, conv_w_ref, conv_b_ref, w_out_ref, g_ffn_ref, w_up_ref,
                   fcw_ref, fcb_ref, w_down_ref, g_ple_ref, w_gate_ref, w_proj_ref, g_final_ref,
                   y_ref, npool_ref, nconv_ref, nffn_ref):
    x = x_ref[...]
    h = _rms(x, g_mix_ref[...]).astype(jnp.bfloat16)

    u = _dot(h, w_in_ref[:, 0:POOL_WIDTH])
    parts = []
    for g, w in enumerate(POOL_WINDOWS):
        sl = slice(g * POOL_GC, (g + 1) * POOL_GC)
        ug = u[:, sl]
        s = ug
        for k in range(1, w):
            s = s + sp_ref[POOL_HIST - k, :, sl]
        cnt = float(min(w, PAST_LEN + 1))
        d = s / cnt - ug
        yp = _dot(d.astype(jnp.bfloat16), w_pool_ref[g]) * pool_scale_ref[:, sl]
        parts.append(yp.astype(jnp.bfloat16))
    for i in range(POOL_HIST - 1):
        npool_ref[i] = sp_ref[i + 1]
    npool_ref[POOL_HIST - 1] = u

    o = POOL_WIDTH
    bg = _dot(h, w_in_ref[:, o:o + CONV_WIDTH])
    cg = _dot(h, w_in_ref[:, o + CONV_WIDTH:o + 2 * CONV_WIDTH])
    hv = _dot(h, w_in_ref[:, o + 2 * CONV_WIDTH:o + 3 * CONV_WIDTH])
    z = cg * hv
    cw = conv_w_ref[...]
    conv_y = sc_ref[0] * cw[0:1, :] + sc_ref[1] * cw[1:2, :] + z * cw[2:3, :] + conv_b_ref[...]
    parts.append((bg * conv_y).astype(jnp.bfloat16))
    nconv_ref[0] = sc_ref[1]
    nconv_ref[1] = z

    mix = jnp.concatenate(parts, axis=-1)
    r1 = x + _dot(mix, w_out_ref[...])
    h2 = _rms(r1, g_ffn_ref[...]).astype(jnp.bfloat16)

    acc = jnp.zeros((x.shape[0], D_MODEL), jnp.float32)
    for c in range(N_CHUNK):
        upre = _dot(h2, w_up_ref[c])
        w = fcw_ref[c]
        ups = []
        for half in range(2):
            col = half * D_FF + c * FF_CHUNK
            hs = slice(half * FF_CHUNK, (half + 1) * FF_CHUNK)
            cur = upre[:, hs]
            prev2 = sf_ref[0, :, col:col + FF_CHUNK]
            prev1 = sf_ref[1, :, col:col + FF_CHUNK]
            ups.append(prev2 * w[0:1, hs] + prev1 * w[1:2, hs] + cur * w[2:3, hs] + fcb_ref[c][:, hs])
            nffn_ref[0, :, col:col + FF_CHUNK] = prev1
            nffn_ref[1, :, col:col + FF_CHUNK] = cur
        act = (ups[0] * jax.nn.sigmoid(ups[0])) * ups[1]
        acc = acc + _dot(act.astype(jnp.bfloat16), w_down_ref[c])

    r2 = r1 + acc
    hg = _rms(r2, g_ple_ref[...]).astype(jnp.bfloat16)
    gate = jax.nn.sigmoid(_dot(hg, w_gate_ref[...]))
    pp = _dot(p_ref[...].astype(jnp.bfloat16), w_proj_ref[...])
    r3 = r2 + gate * pp
    y_ref[...] = _rms(r3, g_final_ref[...])


def _chunk_cols(a):
    lead = a.shape[:-1]
    a = a.reshape(lead + (2, N_CHUNK, FF_CHUNK))
    a = jnp.moveaxis(a, -2, 0)
    return a.reshape((N_CHUNK,) + lead + (2 * FF_CHUNK,))


def _resident(shape):
    nd = len(shape)
    return pl.BlockSpec(shape, lambda b, j: (0,) * nd, pipeline_mode=pl.Buffered(1))


def kernel(x_prompt, x_sample, state_pool, state_conv, state_ffn, p_prompt, p_sample, g_mix, w_in, w_pool, pool_scale, conv_w, conv_b, w_out, g_ffn, w_up, ffn_conv_w, ffn_conv_b, w_down, g_ple, w_ple_gate, w_ple_proj, g_final):
    assert g_mix.shape[0] == 1, "single-layer step"
    nb, tp, _ = x_prompt.shape
    ns = x_sample.shape[0]
    assert x_sample.shape[1] == 1 and tp % TIME_TILE == 0
    bf = jnp.bfloat16
    f32 = jnp.float32

    weights = (
        g_mix[0].reshape(1, D_MODEL),
        w_in[0].astype(bf),
        w_pool[0].astype(bf),
        pool_scale[0].reshape(1, POOL_WIDTH),
        conv_w[0],
        conv_b[0].reshape(1, CONV_WIDTH),
        w_out[0].astype(bf),
        g_ffn[0].reshape(1, D_MODEL),
        _chunk_cols(w_up[0]).astype(bf),
        _chunk_cols(ffn_conv_w[0]),
        _chunk_cols(ffn_conv_b[0].reshape(1, 2 * D_FF)),
        w_down[0].astype(bf).reshape(N_CHUNK, FF_CHUNK, D_MODEL),
        g_ple[0].reshape(1, D_MODEL),
        w_ple_gate[0].astype(bf),
        w_ple_proj[0].astype(bf),
        g_final.reshape(1, D_MODEL),
    )

    tm = TIME_TILE
    y_p, npool_p, nconv_p, nffn_c = pl.pallas_call(
        _prompt_kernel,
        grid=(nb, tp // tm),
        in_specs=[pl.BlockSpec((1, tm, D_MODEL), lambda b, j: (b, j, 0)),
                  pl.BlockSpec((1, tm, PLE_DIM), lambda b, j: (b, j, 0))]
                 + [_resident(w.shape) for w in weights],
        out_specs=[pl.BlockSpec((1, tm, D_MODEL), lambda b, j: (b, j, 0)),
                   pl.BlockSpec((1, POOL_HIST, POOL_WIDTH), lambda b, j: (b, 0, 0)),
                   pl.BlockSpec((1, CONV_K - 1, CONV_WIDTH), lambda b, j: (b, 0, 0)),
                   pl.BlockSpec((1, N_CHUNK, CONV_PAD, 2 * FF_CHUNK), lambda b, j: (b, 0, 0, 0))],
        out_shape=[jax.ShapeDtypeStruct((nb, tp, D_MODEL), f32),
                   jax.ShapeDtypeStruct((nb, POOL_HIST, POOL_WIDTH), f32),
                   jax.ShapeDtypeStruct((nb, CONV_K - 1, CONV_WIDTH), f32),
                   jax.ShapeDtypeStruct((nb, N_CHUNK, CONV_PAD, 2 * FF_CHUNK), f32)],
        scratch_shapes=[pltpu.VMEM((POOL_PAD + tm, POOL_WIDTH), f32),
                        pltpu.VMEM((CONV_PAD + tm, CONV_WIDTH), f32),
                        pltpu.VMEM((tm, D_MODEL), bf),
                        pltpu.VMEM((tm, D_MODEL), f32),
                        pltpu.VMEM((tm, D_MODEL), bf),
                        pltpu.VMEM((tm, D_MODEL), f32),
                        pltpu.VMEM((CONV_PAD + tm, 2 * FF_CHUNK), f32),
                        pltpu.VMEM((N_CHUNK, CONV_PAD, 2 * FF_CHUNK), f32)],
        compiler_params=pltpu.CompilerParams(
            dimension_semantics=("arbitrary", "arbitrary"),
            vmem_limit_bytes=V7X_VMEM_LIMIT_BYTES),
        name="prompt_step",
    )(x_prompt, p_prompt[0], *weights)

    nffn_p = nffn_c[:, :, CONV_PAD - (CONV_K - 1):, :].reshape(nb, N_CHUNK, CONV_K - 1, 2, FF_CHUNK)
    nffn_p = nffn_p.transpose(0, 2, 3, 1, 4).reshape(nb, CONV_K - 1, 2 * D_FF)

    sp_t = state_pool[0].transpose(1, 0, 2)
    sc_t = state_conv[0].transpose(1, 0, 2)
    sf_t = state_ffn[0].transpose(1, 0, 2)
    y_s, npool_t, nconv_t, nffn_t = pl.pallas_call(
        _sample_kernel,
        out_shape=[jax.ShapeDtypeStruct((ns, D_MODEL), f32),
                   jax.ShapeDtypeStruct(sp_t.shape, f32),
                   jax.ShapeDtypeStruct(sc_t.shape, f32),
                   jax.ShapeDtypeStruct(sf_t.shape, f32)],
        compiler_params=pltpu.CompilerParams(vmem_limit_bytes=V7X_VMEM_LIMIT_BYTES),
        name="sample_step",
    )(x_sample[:, 0, :], p_sample[0, :, 0, :], sp_t, sc_t, sf_t, *weights)

    return (y_p, y_s[:, None, :],
            npool_p[None], nconv_p[None], nffn_p[None],
            npool_t.transpose(1, 0, 2)[None], nconv_t.transpose(1, 0, 2)[None],
            nffn_t.transpose(1, 0, 2)[None])
```
